```python
import jax, jax.numpy as jnp
from jax import lax
import numpy as np

D_MODEL = 1024
BATCH = 8
SEQ = 2048
DEPTH = 2

N_META = 16
FOX_HEADS = 8
FOX_HEAD_DIM = 64
FOX_WIDTH = FOX_HEADS * FOX_HEAD_DIM
Q_BLOCK = 128
CONV_CH = D_MODEL - FOX_WIDTH
CONV_WIDTH = 31
IN_COLS = 3 * FOX_WIDTH + FOX_HEADS + 2 * CONV_CH
POOL_WINDOWS = (2, 4, 8, 16)
N_POOL_GROUPS = len(POOL_WINDOWS)
POOL_GROUP = D_MODEL // N_POOL_GROUPS
D_FF = 2816
FFN_CONV_WIDTH = 3
RMS_EPS = 1e-6
LN_EPS = 1e-5
N_EVEN = (DEPTH + 1) // 2
N_ODD = DEPTH // 2

kernel_name = "fox_conformer_pool_hybrid_block"


def rms_norm(x, g):
    xf = x.astype(jnp.float32)
    y = xf * lax.rsqrt(jnp.mean(xf * xf, axis=-1, keepdims=True) + RMS_EPS)
    return (y * g.astype(jnp.float32)).astype(x.dtype)


def layer_norm(x, g, b):
    xf = x.astype(jnp.float32)
    mu = jnp.mean(xf, axis=-1, keepdims=True)
    var = jnp.mean(jnp.square(xf - mu), axis=-1, keepdims=True)
    y = (xf - mu) * lax.rsqrt(var + LN_EPS)
    return (y * g.astype(jnp.float32) + b.astype(jnp.float32)).astype(x.dtype)


def causal_depthwise_conv(x, w, b):
    K, C = w.shape
    y = lax.conv_general_dilated(
        x, w[:, None, :].astype(x.dtype), window_strides=(1,), padding=[(K - 1, 0)],
        dimension_numbers=('NWC', 'WIO', 'NWC'), feature_group_count=C)
    return y + b.astype(x.dtype)


def forgetting_attention(q, k, v, log_f):
    B, L, H, Dh = q.shape
    n_blk = (L - N_META) // Q_BLOCK
    qf, kf, vf = (a.astype(jnp.float32) for a in (q, k, v))
    c = jnp.transpose(jnp.cumsum(log_f, axis=1), (0, 2, 1))
    scale = Dh ** -0.5
    pos = jnp.arange(L)

    def attend(q_blk, c_q, q_pos):
        s = jnp.einsum('bqhd,bkhd->bhqk', q_blk, kf) * scale
        s = s + c_q[..., :, None] - c[..., None, :]
        s = jnp.where(pos[None, :] <= q_pos[:, None], s, -jnp.inf)
        p = jax.nn.softmax(s, axis=-1)
        return jnp.einsum('bhqk,bkhd->bqhd', p, vf)

    out_meta = attend(qf[:, :N_META], c[:, :, :N_META], pos[:N_META])
    q_r = jnp.transpose(qf[:, N_META:].reshape(B, n_blk, Q_BLOCK, H, Dh), (1, 0, 2, 3, 4))
    c_r = jnp.transpose(c[:, :, N_META:].reshape(B, H, n_blk, Q_BLOCK), (2, 0, 1, 3))
    p_r = pos[N_META:].reshape(n_blk, Q_BLOCK)
    out_r = lax.map(lambda a: attend(*a), (q_r, c_r, p_r))
    out_r = jnp.transpose(out_r, (1, 0, 2, 3, 4)).reshape(B, L - N_META, H, Dh)
    out = jnp.concatenate([out_meta, out_r], axis=1)
    return out.reshape(B, L, H * Dh)


def fox_conformer_mixer(h, w_in, b_f, conv_w, conv_b, ln_g, ln_b, w_out):
    B, L, _ = h.shape
    proj = h @ w_in.astype(h.dtype)
    q, k, v, f_logit, glu = jnp.split(
        proj, [FOX_WIDTH, 2 * FOX_WIDTH, 3 * FOX_WIDTH, 3 * FOX_WIDTH + FOX_HEADS], axis=-1)
    log_f = jax.nn.log_sigmoid(f_logit.astype(jnp.float32) + b_f.astype(jnp.float32))
    shp = (B, L, FOX_HEADS, FOX_HEAD_DIM)
    attn = forgetting_attention(q.reshape(shp), k.reshape(shp), v.reshape(shp), log_f).astype(h.dtype)
    a, g = jnp.split(glu, 2, axis=-1)
    u = a * jax.nn.sigmoid(g)
    u = causal_depthwise_conv(u, conv_w, conv_b)
    u = jax.nn.silu(layer_norm(u, ln_g, ln_b))
    return jnp.concatenate([attn, u], axis=-1) @ w_out.astype(h.dtype)


def multiscale_pool_mixer(h, pool_w, pool_b, pool_scale):
    B, L, D = h.shape
    hf = h.astype(jnp.float32).reshape(B, L, N_POOL_GROUPS, POOL_GROUP)
    cs = jnp.cumsum(hf, axis=1)
    n_seen = jnp.arange(1, L + 1)
    outs = []
    for gi, w in enumerate(POOL_WINDOWS):
        csg = cs[:, :, gi]
        lag = jnp.pad(csg, ((0, 0), (w, 0), (0, 0)))[:, :L]
        cnt = jnp.minimum(n_seen, w).astype(jnp.float32)[None, :, None]
        outs.append((csg - lag) / cnt - hf[:, :, gi])
    d = jnp.stack(outs, axis=2)
    y = jnp.einsum('blgc,gcd->blgd', d, pool_w.astype(jnp.float32)) + pool_b.astype(jnp.float32)
    return (y.reshape(B, L, D) * pool_scale.astype(jnp.float32)).astype(h.dtype)


def conv_glu_ffn(h, w_up, conv_w, conv_b, w_down):
    u = h @ w_up.astype(h.dtype)
    u = causal_depthwise_conv(u, conv_w, conv_b)
    gate, val = jnp.split(u, 2, axis=-1)
    return (jax.nn.silu(gate) * val) @ w_down.astype(h.dtype)


def setup_inputs(seed: int = 0) -> dict:
    key = jax.random.key(seed)
    ks = jax.random.split(key, 24)
    nrm = lambda k, shp, s: jax.random.normal(k, shp, jnp.float32) * s
    D = D_MODEL
    return {
        "x": nrm(ks[0], (BATCH, SEQ, D), 1.0),
        "meta_tokens": nrm(ks[1], (N_META, D), 1.0),
        "mix_norm_even": 1.0 + nrm(ks[2], (N_EVEN, D), 0.02),
        "w_in": nrm(ks[3], (N_EVEN, D, IN_COLS), D ** -0.5),
        "b_f": jax.random.uniform(ks[4], (N_EVEN, FOX_HEADS), jnp.float32, 2.0, 5.0),
        "conv_w": nrm(ks[5], (N_EVEN, CONV_WIDTH, CONV_CH), CONV_WIDTH ** -0.5),
        "conv_b": nrm(ks[6], (N_EVEN, CONV_CH), 0.02),
        "ln_g": 1.0 + nrm(ks[7], (N_EVEN, CONV_CH), 0.02),
        "ln_b": nrm(ks[8], (N_EVEN, CONV_CH), 0.02),
        "w_out": nrm(ks[9], (N_EVEN, FOX_WIDTH + CONV_CH, D), (FOX_WIDTH + CONV_CH) ** -0.5),
        "mix_norm_odd": 1.0 + nrm(ks[10], (N_ODD, D), 0.02),
        "pool_w": nrm(ks[11], (N_ODD, N_POOL_GROUPS, POOL_GROUP, POOL_GROUP), POOL_GROUP ** -0.5),
        "pool_b": nrm(ks[12], (N_ODD, N_POOL_GROUPS, POOL_GROUP), 0.02),
        "pool_scale": 0.5 + nrm(ks[13], (N_ODD, D), 0.1),
        "ffn_norm": 1.0 + nrm(ks[14], (DEPTH, D), 0.02),
        "w_up": nrm(ks[15], (DEPTH, D, 2 * D_FF), D ** -0.5),
        "ffn_conv_w": nrm(ks[16], (DEPTH, FFN_CONV_WIDTH, 2 * D_FF), FFN_CONV_WIDTH ** -0.5),
        "ffn_conv_b": nrm(ks[17], (DEPTH, 2 * D_FF), 0.02),
        "w_down": nrm(ks[18], (DEPTH, D_FF, D), D_FF ** -0.5),
        "final_norm": 1.0 + nrm(ks[19], (D,), 0.02),
    }


def reference(x, meta_tokens, mix_norm_even, w_in, b_f, conv_w, conv_b, ln_g, ln_b, w_out,
              mix_norm_odd, pool_w, pool_b, pool_scale,
              ffn_norm, w_up, ffn_conv_w, ffn_conv_b, w_down, final_norm):
    B = x.shape[0]
    meta = jnp.broadcast_to(meta_tokens.astype(x.dtype)[None], (B, N_META, x.shape[-1]))
    h = jnp.concatenate([meta, x], axis=1)
    for i in range(DEPTH):
        j = i // 2
        if i % 2 == 0:
            h = h + fox_conformer_mixer(rms_norm(h, mix_norm_even[j]), w_in[j], b_f[j], conv_w[j],
                                        conv_b[j], ln_g[j], ln_b[j], w_out[j])
        else:
            h = h + multiscale_pool_mixer(rms_norm(h, mix_norm_odd[j]), pool_w[j], pool_b[j],
                                          pool_scale[j])
        h = h + conv_glu_ffn(rms_norm(h, ffn_norm[i]), w_up[i], ffn_conv_w[i], ffn_conv_b[i],
                             w_down[i])
    h = rms_norm(h, final_norm)
    return h[:, N_META:]
```

```python
import functools

import jax
import jax.numpy as jnp
from jax import lax
from jax.experimental import pallas as pl
from jax.experimental.pallas import tpu as pltpu

F32 = jnp.float32
BF16 = jnp.bfloat16

D_MODEL = 1024
N_META = 16
FOX_HEADS = 8
FOX_HEAD_DIM = 64
FOX_WIDTH = FOX_HEADS * FOX_HEAD_DIM
CONV_CH = D_MODEL - FOX_WIDTH
CONV_WIDTH = 31
POOL_WINDOWS = (2, 4, 8, 16)
POOL_GROUP = D_MODEL // len(POOL_WINDOWS)
D_FF = 2816
RMS_EPS = 1e-6
LN_EPS = 1e-5

LANES = 128
META_PAD = 128
IN_COLS_PAD = 5 * FOX_WIDTH + LANES
FFN_CHUNKS = 2
FFN_CHUNK = D_FF // FFN_CHUNKS
CONV_HALO = 32
POOL_HALO = 16
FFN_HALO = 8
NEG_BIG = -1e30
VMEM_LIMIT = 56 * 1024 * 1024


def _rms(x, g):
    ms = jnp.mean(x * x, axis=-1, keepdims=True)
    return x * lax.rsqrt(ms + RMS_EPS) * g


def _const_spec(shape, single_buffer=False):
    nd = len(shape)
    kw = {"pipeline_mode": pl.Buffered(1)} if single_buffer else {}
    return pl.BlockSpec(shape, lambda *_: (0,) * nd, **kw)


def _params(n_grid):
    return pltpu.CompilerParams(
        dimension_semantics=("arbitrary",) * n_grid, vmem_limit_bytes=VMEM_LIMIT)


def _mixin_kernel(h_ref, g_ref, w_ref, bf_ref, c0_ref,
                  q_ref, k_ref, v_ref, u_ref, ccol_ref, crow_ref, carry, *, T):
    t = pl.program_id(1)

    @pl.when(t == 0)
    def _():
        carry[...] = c0_ref[...]

    hn = _rms(h_ref[0], g_ref[...]).astype(BF16)
    p = jnp.dot(hn, w_ref[...], preferred_element_type=F32)
    W = FOX_WIDTH
    q_ref[0] = (p[:, 0:W] * (FOX_HEAD_DIM ** -0.5)).astype(BF16)
    k_ref[0] = p[:, W:2 * W].astype(BF16)
    v_ref[0] = p[:, 2 * W:3 * W].astype(BF16)
    u_ref[0] = p[:, 3 * W:4 * W] * jax.nn.sigmoid(p[:, 4 * W:5 * W])
    fl = p[:, 5 * W:5 * W + LANES] + bf_ref[...]
    logf = jnp.minimum(fl, 0.0) - jnp.log1p(jnp.exp(-jnp.abs(fl)))
    lane = lax.broadcasted_iota(jnp.int32, (T, LANES), 1)
    logf = jnp.where(lane < FOX_HEADS, logf, 0.0)
    hi = logf.astype(BF16)
    r1 = logf - hi.astype(F32)
    mid = r1.astype(BF16)
    lo = (r1 - mid.astype(F32)).astype(BF16)
    row = lax.broadcasted_iota(jnp.int32, (T, T), 0)
    col = lax.broadcasted_iota(jnp.int32, (T, T), 1)
    tri = jnp.where(row >= col, 1.0, 0.0).astype(BF16)
    c = (jnp.dot(tri, hi, preferred_element_type=F32)
         + jnp.dot(tri, mid, preferred_element_type=F32)
         + jnp.dot(tri, lo, preferred_element_type=F32)) + carry[...]
    carry[...] = c[T - 1:T, :]
    ccol_ref[0] = c
    crow_ref[0] = c.T[0:FOX_HEADS, :]


def _mix_in(h, g, w_in_r, bf_pad, c0, *, T):
    nb, L, D = h.shape
    grid = (nb, L // T)
    tile = lambda w: pl.BlockSpec((1, T, w), lambda b, t: (b, t, 0))
    return pl.pallas_call(
        functools.partial(_mixin_kernel, T=T),
        grid=grid,
        in_specs=[tile(D), _const_spec((1, D)), _const_spec((D, IN_COLS_PAD)),
                  _const_spec((1, LANES)), _const_spec((1, LANES))],
        out_specs=[tile(FOX_WIDTH), tile(FOX_WIDTH), tile(FOX_WIDTH), tile(CONV_CH),
                   tile(LANES), pl.BlockSpec((1, FOX_HEADS, T), lambda b, t: (b, 0, t))],
        out_shape=[jax.ShapeDtypeStruct((nb, L, FOX_WIDTH), BF16)] * 3
        + [jax.ShapeDtypeStruct((nb, L, CONV_CH), F32),
           jax.ShapeDtypeStruct((nb, L, LANES), F32),
           jax.ShapeDtypeStruct((nb, FOX_HEADS, L), F32)],
        scratch_shapes=[pltpu.VMEM((1, LANES), F32)],
        compiler_params=_params(2),
        name="mix_in",
    )(h, g, w_in_r, bf_pad, c0)


def _attn_kernel(*refs, L, TQ, n_meta):
    if n_meta:
        q_ref, k_ref, v_ref, ccol_ref, crow_ref, km_ref, vm_ref, crm_ref, o_ref = refs
    else:
        q_ref, k_ref, v_ref, ccol_ref, crow_ref, o_ref = refs
    hp = pl.program_id(1)
    lane = lax.broadcasted_iota(jnp.int32, (TQ, LANES), 1)
    first = lane < FOX_HEAD_DIM
    row = lax.broadcasted_iota(jnp.int32, (TQ, TQ), 0)
    col = lax.broadcasted_iota(jnp.int32, (TQ, TQ), 1)
    causal = row >= col
    dn_nt = (((1,), (1,)), ((), ()))

    def q_body(qi, _):
        r0 = pl.multiple_of(qi * TQ, TQ)
        q2 = q_ref[0, pl.ds(r0, TQ), :]
        cc = ccol_ref[0, pl.ds(r0, TQ), :]
        zero = jnp.zeros_like(q2)
        qs = (jnp.where(first, q2, zero), jnp.where(first, zero, q2))
        cq = tuple(jnp.sum(jnp.where(lane == 2 * hp + h, cc, 0.0), axis=1, keepdims=True)
                   for h in range(2))

        def tile(carry, kt, vt, ck, mask):
            out = []
            for h in range(2):
                m, l, acc = carry[h]
                s = lax.dot_general(qs[h], kt, dn_nt, preferred_element_type=F32)
                s = s + (cq[h] - ck[h])
                if mask is not None:
                    s = jnp.where(mask, s, NEG_BIG)
                m_new = jnp.maximum(m, jnp.max(s, axis=1, keepdims=True))
                alpha = jnp.exp(m - m_new)
                p = jnp.exp(s - m_new)
                l = alpha * l + jnp.sum(p, axis=1, keepdims=True)
                acc = alpha * acc + jnp.dot(p.astype(BF16), vt, preferred_element_type=F32)
                out.append((m_new, l, acc))
            return tuple(out)

        def kv_tile(j):
            c0 = pl.multiple_of(j * TQ, TQ)
            kt = k_ref[0, pl.ds(c0, TQ), :]
            vt = v_ref[0, pl.ds(c0, TQ), :]
            ck = tuple(crow_ref[0, pl.ds(2 * hp + h, 1), pl.ds(c0, TQ)] for h in range(2))
            return kt, vt, ck

        init = tuple((jnp.full((TQ, 1), NEG_BIG, F32), jnp.zeros((TQ, 1), F32),
                      jnp.zeros((TQ, LANES), F32)) for _ in range(2))
        carry = tile(init, *kv_tile(qi), causal)
        carry = lax.fori_loop(0, qi, lambda j, c: tile(c, *kv_tile(j), None), carry)
        if n_meta:
            ckm = tuple(crm_ref[0, pl.ds(2 * hp + h, 1), :] for h in range(2))
            carry = tile(carry, km_ref[0], vm_ref[0], ckm, lane < n_meta)
        o = jnp.where(first, carry[0][2] / carry[0][1], carry[1][2] / carry[1][1])
        o_ref[0, pl.ds(r0, TQ), :] = o.astype(BF16)
        return 0

    lax.fori_loop(0, L // TQ, q_body, 0)


def _attention(q, k, v, ccol, crow, meta=None, *, TQ):
    nb, L, _ = q.shape
    n_hp = FOX_WIDTH // LANES
    seq = pl.BlockSpec((1, L, LANES), lambda b, hp: (b, 0, hp))
    in_specs = [seq, seq, seq,
                pl.BlockSpec((1, L, LANES), lambda b, hp: (b, 0, 0)),
                pl.BlockSpec((1, FOX_HEADS, L), lambda b, hp: (b, 0, 0))]
    args = [q, k, v, ccol, crow]
    if meta is not None:
        mseq = pl.BlockSpec((1, META_PAD, LANES), lambda b, hp: (0, 0, hp))
        in_specs += [mseq, mseq, pl.BlockSpec((1, FOX_HEADS, META_PAD), lambda b, hp: (0, 0, 0))]
        args += list(meta)
    return pl.pallas_call(
        functools.partial(_attn_kernel, L=L, TQ=TQ, n_meta=N_META if meta is not None else 0),
        grid=(nb, n_hp),
        in_specs=in_specs,
        out_specs=seq,
        out_shape=jax.ShapeDtypeStruct((nb, L, FOX_WIDTH), BF16),
        compiler_params=_params(2),
        name="fox_attention",
    )(*args)


def _mixout_kernel(attn_ref, u_ref, h_ref, hist_ref, cw_ref, cb_ref, lg_ref, lb_ref, wo_ref,
                   out_ref, ubuf, *, T):
    t = pl.program_id(1)

    @pl.when(t == 0)
    def _():
        ubuf[0:CONV_HALO, :] = hist_ref[...]

    ubuf[CONV_HALO:CONV_HALO + T, :] = u_ref[0]
    acc = jnp.zeros((T, CONV_CH), F32) + cb_ref[...]
    base = CONV_HALO - (CONV_WIDTH - 1)
    for kk in range(CONV_WIDTH):
        acc = acc + cw_ref[kk:kk + 1, :] * ubuf[base + kk:base + kk + T, :]
    ubuf[0:CONV_HALO, :] = ubuf[T:T + CONV_HALO, :]
    mu = jnp.mean(acc, axis=-1, keepdims=True)
    d = acc - mu
    var = jnp.mean(d * d, axis=-1, keepdims=True)
    y = d * lax.rsqrt(var + LN_EPS) * lg_ref[...] + lb_ref[...]
    z = y * jax.nn.sigmoid(y)
    cat = jnp.concatenate([attn_ref[0], z.astype(BF16)], axis=1)
    out_ref[0] = h_ref[0] + jnp.dot(cat, wo_ref[...], preferred_element_type=F32)


def _mix_out(attn, u, h, hist, cw, cb, lg, lb, wo, *, T):
    nb, L, D = h.shape
    tile = lambda w: pl.BlockSpec((1, T, w), lambda b, t: (b, t, 0))
    return pl.pallas_call(
        functools.partial(_mixout_kernel, T=T),
        grid=(nb, L // T),
        in_specs=[tile(FOX_WIDTH), tile(CONV_CH), tile(D), _const_spec((CONV_HALO, CONV_CH)),
                  _const_spec((CONV_HALO, CONV_CH)), _const_spec((1, CONV_CH)),
                  _const_spec((1, CONV_CH)), _const_spec((1, CONV_CH)), _const_spec((D, D))],
        out_specs=tile(D),
        out_shape=jax.ShapeDtypeStruct((nb, L, D), F32),
        scratch_shapes=[pltpu.VMEM((CONV_HALO + T, CONV_CH), F32)],
        compiler_params=_params(2),
        name="mix_out",
    )(attn, u, h, hist, cw, cb, lg, lb, wo)


def _pool_kernel(h_ref, g_ref, hist_ref, pw_ref, pb_ref, ps_ref, out_ref, tail_ref, buf,
                 *, T, pos0, n_valid):
    t = pl.program_id(1)

    @pl.when(t == 0)
    def _():
        buf[0:POOL_HALO, :] = hist_ref[...]

    x = h_ref[0]
    buf[POOL_HALO:POOL_HALO + T, :] = _rms(x, g_ref[...])
    n_seen = lax.broadcasted_iota(jnp.int32, (T, 1), 0) + (t * T + pos0 + 1)
    G = POOL_GROUP
    for gi, w in enumerate(POOL_WINDOWS):
        cs = slice(gi * G, (gi + 1) * G)
        cur = buf[POOL_HALO:POOL_HALO + T, cs]
        s = cur
        for j in range(1, w):
            s = s + buf[POOL_HALO - j:POOL_HALO - j + T, cs]
        inv_cnt = 1.0 / jnp.minimum(n_seen, w).astype(F32)
        d = s * inv_cnt - cur
        y = jnp.dot(d.astype(BF16), pw_ref[gi], preferred_element_type=F32) + pb_ref[:, cs]
        out_ref[0, :, cs] = x[:, cs] + y * ps_ref[:, cs]
    tail_ref[...] = buf[n_valid:n_valid + POOL_HALO, :]
    buf[0:POOL_HALO, :] = buf[T:T + POOL_HALO, :]


def _pool_mixer(h, g, hist, pw, pb, ps, *, T, pos0, n_valid):
    nb, L, D = h.shape
    tile = pl.BlockSpec((1, T, D), lambda b, t: (b, t, 0))
    return pl.pallas_call(
        functools.partial(_pool_kernel, T=T, pos0=pos0, n_valid=n_valid),
        grid=(nb, L // T),
        in_specs=[tile, _const_spec((1, D)), _const_spec((POOL_HALO, D)),
                  _const_spec((len(POOL_WINDOWS), POOL_GROUP, POOL_GROUP)),
                  _const_spec((1, D)), _const_spec((1, D))],
        out_specs=[tile, _const_spec((POOL_HALO, D))],
        out_shape=[jax.ShapeDtypeStruct((nb, L, D), F32),
                   jax.ShapeDtypeStruct((POOL_HALO, D), F32)],
        scratch_shapes=[pltpu.VMEM((POOL_HALO + T, D), F32)],
        compiler_params=_params(2),
        name="pool_mixer",
    )(h, g, hist, pw, pb, ps)


def _ffn_kernel(*refs, T, n_valid, final_norm):
    if final_norm:
        (h_ref, g_ref, wup_ref, cw_ref, cb_ref, wd_ref, hist_ref, fn_ref,
         out_ref, tail_ref, halo, gbuf, vbuf) = refs
    else:
        (h_ref, g_ref, wup_ref, cw_ref, cb_ref, wd_ref, hist_ref,
         out_ref, tail_ref, halo, gbuf, vbuf) = refs
    t = pl.program_id(1)

    @pl.when(t == 0)
    def _():
        halo[...] = hist_ref[...]

    x = h_ref[0]
    hn = _rms(x, g_ref[...]).astype(BF16)
    H = FFN_HALO
    y = x
    for c in range(FFN_CHUNKS):
        halves = []
        for buf, c0 in ((gbuf, c * FFN_CHUNK), (vbuf, D_FF + c * FFN_CHUNK)):
            cs = slice(c0, c0 + FFN_CHUNK)
            buf[0:H, :] = halo[:, cs]
            buf[H:H + T, :] = jnp.dot(hn, wup_ref[:, cs], preferred_element_type=F32)
            conv = (cw_ref[0:1, cs] * buf[H - 2:H - 2 + T, :]
                    + cw_ref[1:2, cs] * buf[H - 1:H - 1 + T, :]
                    + cw_ref[2:3, cs] * buf[H:H + T, :] + cb_ref[:, cs])
            halves.append(conv)
            halo[:, cs] = buf[T:T + H, :]
            tail_ref[:, cs] = buf[n_valid:n_valid + H, :]
        gate, val = halves
        act = (gate * jax.nn.sigmoid(gate) * val).astype(BF16)
        y = y + jnp.dot(act, wd_ref[c * FFN_CHUNK:(c + 1) * FFN_CHUNK, :],
                        preferred_element_type=F32)
    if final_norm:
        y = _rms(y, fn_ref[...])
    out_ref[0] = y


def _ffn(h, g, wup, cw, cb, wd, hist, fn=None, *, T, n_valid):
    nb, L, D = h.shape
    tile = pl.BlockSpec((1, T, D), lambda b, t: (b, t, 0))
    in_specs = [tile, _const_spec((1, D)), _const_spec((D, 2 * D_FF), True),
                _const_spec((8, 2 * D_FF)), _const_spec((1, 2 * D_FF)),
                _const_spec((D_FF, D), True), _const_spec((FFN_HALO, 2 * D_FF))]
    args = [h, g, wup, cw, cb, wd, hist]
    if fn is not None:
        in_specs.append(_const_spec((1, D)))
        args.append(fn)
    return pl.pallas_call(
        functools.partial(_ffn_kernel, T=T, n_valid=n_valid, final_norm=fn is not None),
        grid=(nb, L // T),
        in_specs=in_specs,
        out_specs=[tile, _const_spec((FFN_HALO, 2 * D_FF))],
        out_shape=[jax.ShapeDtypeStruct((nb, L, D), F32),
                   jax.ShapeDtypeStruct((FFN_HALO, 2 * D_FF), F32)],
        scratch_shapes=[pltpu.VMEM((FFN_HALO, 2 * D_FF), F32),
                        pltpu.VMEM((FFN_HALO + T, FFN_CHUNK), F32),
                        pltpu.VMEM((FFN_HALO + T, FFN_CHUNK), F32)],
        compiler_params=_params(2),
        name="conv_glu_ffn",
    )(*args)


def _row(v):
    return v.reshape(1, -1).astype(F32)


def _pad_rows(a, rows):
    return jnp.pad(a.astype(F32), ((0, rows - a.shape[0]), (0, 0)))


def kernel(x, meta_tokens, mix_norm_even, w_in, b_f, conv_w, conv_b, ln_g, ln_b, w_out,
           mix_norm_odd, pool_w, pool_b, pool_scale, ffn_norm, w_up, ffn_conv_w, ffn_conv_b,
           w_down, final_norm):
    W = FOX_WIDTH
    TM = META_PAD
    T_MAIN = 512
    T_ATTN = 256

    wi = w_in[0]
    w_in_r = jnp.concatenate(
        [wi[:, 0:3 * W], wi[:, 3 * W + FOX_HEADS:],
         jnp.pad(wi[:, 3 * W:3 * W + FOX_HEADS], ((0, 0), (0, LANES - FOX_HEADS)))],
        axis=1).astype(BF16)
    bf_pad = jnp.pad(b_f[0].astype(F32), (0, LANES - FOX_HEADS)).reshape(1, LANES)
    cw31 = _pad_rows(conv_w[0], CONV_HALO)
    wo = w_out[0].astype(BF16)
    pw = pool_w[0].astype(BF16)
    wup = [w_up[i].astype(BF16) for i in range(2)]
    wd = [w_down[i].astype(BF16) for i in range(2)]
    cw3 = [_pad_rows(ffn_conv_w[i], 8) for i in range(2)]

    hm = jnp.pad(meta_tokens.astype(F32), ((0, TM - N_META), (0, 0)))[None]
    hr = x.astype(F32)

    zero_c = jnp.zeros((1, LANES), F32)
    zero_ffn = jnp.zeros((FFN_HALO, 2 * D_FF), F32)

    g0 = _row(mix_norm_even[0])
    qm, km, vm, um, ccol_m, crow_m = _mix_in(hm, g0, w_in_r, bf_pad, zero_c, T=TM)
    c0 = ccol_m[0, N_META - 1:N_META, :]
    q, k, v, u, ccol, crow = _mix_in(hr, g0, w_in_r, bf_pad, c0, T=T_MAIN)
    attn_m = _attention(qm, km, vm, ccol_m, crow_m, TQ=TM)
    attn = _attention(q, k, v, ccol, crow, meta=(km, vm, crow_m), TQ=T_ATTN)
    mo_args = (cw31, _row(conv_b[0]), _row(ln_g[0]), _row(ln_b[0]), wo)
    hm = _mix_out(attn_m, um, hm, jnp.zeros((CONV_HALO, CONV_CH), F32), *mo_args, T=TM)
    hist_u = jnp.concatenate(
        [jnp.zeros((CONV_HALO - N_META, CONV_CH), F32), um[0, 0:N_META]], axis=0)
    hr = _mix_out(attn, u, hr, hist_u, *mo_args, T=T_MAIN)

    f_args = (_row(ffn_norm[0]), wup[0], cw3[0], _row(ffn_conv_b[0]), wd[0])
    hm, tail = _ffn(hm, *f_args, zero_ffn, T=TM, n_valid=N_META)
    hr, _ = _ffn(hr, *f_args, tail, T=T_MAIN, n_valid=T_MAIN)

    p_args = (_row(mix_norm_odd[0]), )
    p_tail = (pw, pool_b[0].reshape(1, -1).astype(F32), _row(pool_scale[0]))
    hm, ptail = _pool_mixer(hm, *p_args, jnp.zeros((POOL_HALO, D_MODEL), F32), *p_tail,
                            T=TM, pos0=0, n_valid=N_META)
    hr, _ = _pool_mixer(hr, *p_args, ptail, *p_tail, T=T_MAIN, pos0=N_META, n_valid=T_MAIN)

    f_args = (_row(ffn_norm[1]), wup[1], cw3[1], _row(ffn_conv_b[1]), wd[1])
    _, tail = _ffn(hm, *f_args, zero_ffn, T=TM, n_valid=N_META)
    out, _ = _ffn(hr, *f_args, tail, _row(final_norm), T=T_MAIN, n_valid=T_MAIN)
    return out.astype(x.dtype)
```

```python
import functools

import jax
import jax.numpy as jnp
from jax import lax
from jax.experimental import pallas as pl
from jax.experimental.pallas import tpu as pltpu

F32 = jnp.float32
BF16 = jnp.bfloat16

D_MODEL = 1024
N_META = 16
FOX_HEADS = 8
FOX_HEAD_DIM = 64
FOX_WIDTH = FOX_HEADS * FOX_HEAD_DIM
CONV_CH = D_MODEL - FOX_WIDTH
CONV_WIDTH = 31
POOL_WINDOWS = (2, 4, 8, 16)
POOL_GROUP = D_MODEL // len(POOL_WINDOWS)
D_FF = 2816
RMS_EPS = 1e-6
LN_EPS = 1e-5

LANES = 128
META_PAD = 128
IN_COLS_PAD = 5 * FOX_WIDTH + LANES
FFN_CHUNKS = 2
FFN_CHUNK = D_FF // FFN_CHUNKS
CONV_HALO = 32
POOL_HALO = 16
FFN_HALO = 8
NEG_BIG = -1e30
VMEM_LIMIT = 56 * 1024 * 1024


def _rms(x, g):
    ms = jnp.mean(x * x, axis=-1, keepdims=True)
    return x * lax.rsqrt(ms + RMS_EPS) * g


def _const_spec(shape, single_buffer=False):
    nd = len(shape)
    kw = {"pipeline_mode": pl.Buffered(1)} if single_buffer else {}
    return pl.BlockSpec(shape, lambda *_: (0,) * nd, **kw)


def _params(n_grid):
    return pltpu.CompilerParams(
        dimension_semantics=("arbitrary",) * n_grid, vmem_limit_bytes=VMEM_LIMIT)


def _mixin_kernel(h_ref, g_ref, w_ref, bf_ref, c0_ref,
                  q_ref, k_ref, v_ref, u_ref, ccol_ref, crow_ref, carry, *, T):
    t = pl.program_id(1)

    @pl.when(t == 0)
    def _():
        carry[...] = c0_ref[...]

    hn = _rms(h_ref[0], g_ref[...]).astype(BF16)
    p = jnp.dot(hn, w_ref[...], preferred_element_type=F32)
    W = FOX_WIDTH
    q_ref[0] = (p[:, 0:W] * (FOX_HEAD_DIM ** -0.5)).astype(BF16)
    k_ref[0] = p[:, W:2 * W].astype(BF16)
    v_ref[0] = p[:, 2 * W:3 * W].astype(BF16)
    u_ref[0] = p[:, 3 * W:4 * W] * jax.nn.sigmoid(p[:, 4 * W:5 * W])
    fl = p[:, 5 * W:5 * W + LANES] + bf_ref[...]
    logf = jnp.minimum(fl, 0.0) - jnp.log1p(jnp.exp(-jnp.abs(fl)))
    lane = lax.broadcasted_iota(jnp.int32, (T, LANES), 1)
    logf = jnp.where(lane < FOX_HEADS, logf, 0.0)
    hi = logf.astype(BF16)
    r1 = logf - hi.astype(F32)
    mid = r1.astype(BF16)
    lo = (r1 - mid.astype(F32)).astype(BF16)
    row = lax.broadcasted_iota(jnp.int32, (T, T), 0)
    col = lax.broadcasted_iota(jnp.int32, (T, T), 1)
    tri = jnp.where(row >= col, 1.0, 0.0).astype(BF16)
    c = (jnp.dot(tri, hi, preferred_element_type=F32)
         + jnp.dot(tri, mid, preferred_element_type=F32)
         + jnp.dot(tri, lo, preferred_element_type=F32)) + carry[...]
    carry[...] = c[T - 1:T, :]
    ccol_ref[0] = c
    crow_ref[0] = c.T[0:FOX_HEADS, :]


def _mix_in(h, g, w_in_r, bf_pad, c0, *, T):
    nb, L, D = h.shape
    grid = (nb, L // T)
    tile = lambda w: pl.BlockSpec((1, T, w), lambda b, t: (b, t, 0))
    return pl.pallas_call(
        functools.partial(_mixin_kernel, T=T),
        grid=grid,
        in_specs=[tile(D), _const_spec((1, D)), _const_spec((D, IN_COLS_PAD)),
                  _const_spec((1, LANES)), _const_spec((1, LANES))],
        out_specs=[tile(FOX_WIDTH), tile(FOX_WIDTH), tile(FOX_WIDTH), tile(CONV_CH),
                   tile(LANES), pl.BlockSpec((1, FOX_HEADS, T), lambda b, t: (b, 0, t))],
        out_shape=[jax.ShapeDtypeStruct((nb, L, FOX_WIDTH), BF16)] * 3
        + [jax.ShapeDtypeStruct((nb, L, CONV_CH), F32),
           jax.ShapeDtypeStruct((nb, L, LANES), F32),
           jax.ShapeDtypeStruct((nb, FOX_HEADS, L), F32)],
        scratch_shapes=[pltpu.VMEM((1, LANES), F32)],
        compiler_params=_params(2),
        name="mix_in",
    )(h, g, w_in_r, bf_pad, c0)


def _attn_kernel(*refs, L, TQ, n_meta):
    if n_meta:
        q_ref, k_ref, v_ref, ccol_ref, crow_ref, km_ref, vm_ref, crm_ref, o_ref = refs
    else:
        q_ref, k_ref, v_ref, ccol_ref, crow_ref, o_ref = refs
    hp = pl.program_id(1)
    lane = lax.broadcasted_iota(jnp.int32, (TQ, LANES), 1)
    first = lane < FOX_HEAD_DIM
    row = lax.broadcasted_iota(jnp.int32, (TQ, TQ), 0)
    col = lax.broadcasted_iota(jnp.int32, (TQ, TQ), 1)
    causal = row >= col
    dn_nt = (((1,), (1,)), ((), ()))

    def scores(qh, kt, cq, ck, mask):
        s = lax.dot_general(qh, kt, dn_nt, preferred_element_type=F32) + (cq - ck)
        return s if mask is None else jnp.where(mask, s, NEG_BIG)

    for qi in range(L // TQ):
        r0 = qi * TQ
        q2 = q_ref[0, r0:r0 + TQ, :]
        cc = ccol_ref[0, r0:r0 + TQ, :]
        zero = jnp.zeros_like(q2)
        outs = []
        for h in range(2):
            qh = jnp.where(first, q2, zero) if h == 0 else jnp.where(first, zero, q2)
            cq = jnp.sum(jnp.where(lane == 2 * hp + h, cc, 0.0), axis=1, keepdims=True)
            hrow = pl.ds(2 * hp + h, 1)
            parts = [(scores(qh, k_ref[0, r0:r0 + TQ, :], cq,
                             crow_ref[0, hrow, r0:r0 + TQ], causal), v_ref[0, r0:r0 + TQ, :])]
            if qi:
                parts.append((scores(qh, k_ref[0, 0:r0, :], cq, crow_ref[0, hrow, 0:r0], None),
                              v_ref[0, 0:r0, :]))
            if n_meta:
                parts.append((scores(qh, km_ref[0], cq, crm_ref[0, hrow, :], lane < n_meta),
                              vm_ref[0]))
            m = functools.reduce(
                jnp.maximum, [jnp.max(s, axis=1, keepdims=True) for s, _ in parts])
            l = 0.0
            acc = 0.0
            for s, vt in parts:
                p = jnp.exp(s - m)
                l = l + jnp.sum(p, axis=1, keepdims=True)
                acc = acc + jnp.dot(p.astype(BF16), vt, preferred_element_type=F32)
            outs.append(acc / l)
        o_ref[0, r0:r0 + TQ, :] = jnp.where(first, outs[0], outs[1]).astype(BF16)


def _attention(q, k, v, ccol, crow, meta=None, *, TQ):
    nb, L, _ = q.shape
    n_hp = FOX_WIDTH // LANES
    seq = pl.BlockSpec((1, L, LANES), lambda b, hp: (b, 0, hp))
    in_specs = [seq, seq, seq,
                pl.BlockSpec((1, L, LANES), lambda b, hp: (b, 0, 0)),
                pl.BlockSpec((1, FOX_HEADS, L), lambda b, hp: (b, 0, 0))]
    args = [q, k, v, ccol, crow]
    if meta is not None:
        mseq = pl.BlockSpec((1, META_PAD, LANES), lambda b, hp: (0, 0, hp))
        in_specs += [mseq, mseq, pl.BlockSpec((1, FOX_HEADS, META_PAD), lambda b, hp: (0, 0, 0))]
        args += list(meta)
    return pl.pallas_call(
        functools.partial(_attn_kernel, L=L, TQ=TQ, n_meta=N_META if meta is not None else 0),
        grid=(nb, n_hp),
        in_specs=in_specs,
        out_specs=seq,
        out_shape=jax.ShapeDtypeStruct((nb, L, FOX_WIDTH), BF16),
        compiler_params=_params(2),
        name="fox_attention",
    )(*args)


def _mixout_kernel(attn_ref, u_ref, h_ref, hist_ref, cw_ref, cb_ref, lg_ref, lb_ref, wo_ref,
                   out_ref, ubuf, *, T):
    t = pl.program_id(1)

    @pl.when(t == 0)
    def _():
        ubuf[0:CONV_HALO, :] = hist_ref[...]

    ubuf[CONV_HALO:CONV_HALO + T, :] = u_ref[0]
    acc = jnp.zeros((T, CONV_CH), F32) + cb_ref[...]
    base = CONV_HALO - (CONV_WIDTH - 1)
    for kk in range(CONV_WIDTH):
        acc = acc + cw_ref[kk:kk + 1, :] * ubuf[base + kk:base + kk + T, :]
    ubuf[0:CONV_HALO, :] = ubuf[T:T + CONV_HALO, :]
    mu = jnp.mean(acc, axis=-1, keepdims=True)
    d = acc - mu
    var = jnp.mean(d * d, axis=-1, keepdims=True)
    y = d * lax.rsqrt(var + LN_EPS) * lg_ref[...] + lb_ref[...]
    z = y * jax.nn.sigmoid(y)
    cat = jnp.concatenate([attn_ref[0], z.astype(BF16)], axis=1)
    out_ref[0] = h_ref[0] + jnp.dot(cat, wo_ref[...], preferred_element_type=F32)


def _mix_out(attn, u, h, hist, cw, cb, lg, lb, wo, *, T):
    nb, L, D = h.shape
    tile = lambda w: pl.BlockSpec((1, T, w), lambda b, t: (b, t, 0))
    return pl.pallas_call(
        functools.partial(_mixout_kernel, T=T),
        grid=(nb, L // T),
        in_specs=[tile(FOX_WIDTH), tile(CONV_CH), tile(D), _const_spec((CONV_HALO, CONV_CH)),
                  _const_spec((CONV_HALO, CONV_CH)), _const_spec((1, CONV_CH)),
                  _const_spec((1, CONV_CH)), _const_spec((1, CONV_CH)), _const_spec((D, D))],
        out_specs=tile(D),
        out_shape=jax.ShapeDtypeStruct((nb, L, D), F32),
        scratch_shapes=[pltpu.VMEM((CONV_HALO + T, CONV_CH), F32)],
        compiler_params=_params(2),
        name="mix_out",
    )(attn, u, h, hist, cw, cb, lg, lb, wo)


def _pool_kernel(h_ref, g_ref, hist_ref, pw_ref, pb_ref, ps_ref, out_ref, tail_ref, buf,
                 *, T, pos0, n_valid):
    t = pl.program_id(1)

    @pl.when(t == 0)
    def _():
        buf[0:POOL_HALO, :] = hist_ref[...]

    x = h_ref[0]
    buf[POOL_HALO:POOL_HALO + T, :] = _rms(x, g_ref[...])
    n_seen = lax.broadcasted_iota(jnp.int32, (T, 1), 0) + (t * T + pos0 + 1)
    G = POOL_GROUP
    for gi, w in enumerate(POOL_WINDOWS):
        cs = slice(gi * G, (gi + 1) * G)
        cur = buf[POOL_HALO:POOL_HALO + T, cs]
        s = cur
        for j in range(1, w):
            s = s + buf[POOL_HALO - j:POOL_HALO - j + T, cs]
        inv_cnt = 1.0 / jnp.minimum(n_seen, w).astype(F32)
        d = s * inv_cnt - cur
        y = jnp.dot(d.astype(BF16), pw_ref[gi], preferred_element_type=F32) + pb_ref[:, cs]
        out_ref[0, :, cs] = x[:, cs] + y * ps_ref[:, cs]
    tail_ref[...] = buf[n_valid:n_valid + POOL_HALO, :]
    buf[0:POOL_HALO, :] = buf[T:T + POOL_HALO, :]


def _pool_mixer(h, g, hist, pw, pb, ps, *, T, pos0, n_valid):
    nb, L, D = h.shape
    tile = pl.BlockSpec((1, T, D), lambda b, t: (b, t, 0))
    return pl.pallas_call(
        functools.partial(_pool_kernel, T=T, pos0=pos0, n_valid=n_valid),
        grid=(nb, L // T),
        in_specs=[tile, _const_spec((1, D)), _const_spec((POOL_HALO, D)),
                  _const_spec((len(POOL_WINDOWS), POOL_GROUP, POOL_GROUP)),
                  _const_spec((1, D)), _const_spec((1, D))],
        out_specs=[tile, _const_spec((POOL_HALO, D))],
        out_shape=[jax.ShapeDtypeStruct((nb, L, D), F32),
                   jax.ShapeDtypeStruct((POOL_HALO, D), F32)],
        scratch_shapes=[pltpu.VMEM((POOL_HALO + T, D), F32)],
        compiler_params=_params(2),
        name="pool_mixer",
    )(h, g, hist, pw, pb, ps)


def _ffn_kernel(*refs, T, n_valid, final_norm):
    if final_norm:
        (h_ref, g_ref, wup_ref, cw_ref, cb_ref, wd_ref, hist_ref, fn_ref,
         out_ref, tail_ref, halo, gbuf, vbuf) = refs
    else:
        (h_ref, g_ref, wup_ref, cw_ref, cb_ref, wd_ref, hist_ref,
         out_ref, tail_ref, halo, gbuf, vbuf) = refs
    t = pl.program_id(1)

    @pl.when(t == 0)
    def _():
        halo[...] = hist_ref[...]

    x = h_ref[0]
    hn = _rms(x, g_ref[...]).astype(BF16)
    H = FFN_HALO
    y = x
    for c in range(FFN_CHUNKS):
        halves = []
        for buf, c0 in ((gbuf, c * FFN_CHUNK), (vbuf, D_FF + c * FFN_CHUNK)):
            cs = slice(c0, c0 + FFN_CHUNK)
            buf[0:H, :] = halo[:, cs]
            buf[H:H + T, :] = jnp.dot(hn, wup_ref[:, cs], preferred_element_type=F32)
            conv = (cw_ref[0:1, cs] * buf[H - 2:H - 2 + T, :]
                    + cw_ref[1:2, cs] * buf[H - 1:H - 1 + T, :]
                    + cw_ref[2:3, cs] * buf[H:H + T, :] + cb_ref[:, cs])
            halves.append(conv)
            halo[:, cs] = buf[T:T + H, :]
            tail_ref[:, cs] = buf[n_valid:n_valid + H, :]
        gate, val = halves
        act = (gate * jax.nn.sigmoid(gate) * val).astype(BF16)
        y = y + jnp.dot(act, wd_ref[c * FFN_CHUNK:(c + 1) * FFN_CHUNK, :],
                        preferred_element_type=F32)
    if final_norm:
        y = _rms(y, fn_ref[...])
    out_ref[0] = y


def _ffn(h, g, wup, cw, cb, wd, hist, fn=None, *, T, n_valid):
    nb, L, D = h.shape
    tile = pl.BlockSpec((1, T, D), lambda b, t: (b, t, 0))
    in_specs = [tile, _const_spec((1, D)), _const_spec((D, 2 * D_FF), True),
                _const_spec((8, 2 * D_FF)), _const_spec((1, 2 * D_FF)),
                _const_spec((D_FF, D), True), _const_spec((FFN_HALO, 2 * D_FF))]
    args = [h, g, wup, cw, cb, wd, hist]
    if fn is not None:
        in_specs.append(_const_spec((1, D)))
        args.append(fn)
    return pl.pallas_call(
        functools.partial(_ffn_kernel, T=T, n_valid=n_valid, final_norm=fn is not None),
        grid=(nb, L // T),
        in_specs=in_specs,
        out_specs=[tile, _const_spec((FFN_HALO, 2 * D_FF))],
        out_shape=[jax.ShapeDtypeStruct((nb, L, D), F32),
                   jax.ShapeDtypeStruct((FFN_HALO, 2 * D_FF), F32)],
        scratch_shapes=[pltpu.VMEM((FFN_HALO, 2 * D_FF), F32),
                        pltpu.VMEM((FFN_HALO + T, FFN_CHUNK), F32),
                        pltpu.VMEM((FFN_HALO + T, FFN_CHUNK), F32)],
        compiler_params=_params(2),
        name="conv_glu_ffn",
    )(*args)


def _row(v):
    return v.reshape(1, -1).astype(F32)


def _pad_rows(a, rows):
    return jnp.pad(a.astype(F32), ((0, rows - a.shape[0]), (0, 0)))


def kernel(x, meta_tokens, mix_norm_even, w_in, b_f, conv_w, conv_b, ln_g, ln_b, w_out,
           mix_norm_odd, pool_w, pool_b, pool_scale, ffn_norm, w_up, ffn_conv_w, ffn_conv_b,
           w_down, final_norm):
    W = FOX_WIDTH
    TM = META_PAD
    T_MAIN = 512
    T_ATTN = 256

    wi = w_in[0]
    w_in_r = jnp.concatenate(
        [wi[:, 0:3 * W], wi[:, 3 * W + FOX_HEADS:],
         jnp.pad(wi[:, 3 * W:3 * W + FOX_HEADS], ((0, 0), (0, LANES - FOX_HEADS)))],
        axis=1).astype(BF16)
    bf_pad = jnp.pad(b_f[0].astype(F32), (0, LANES - FOX_HEADS)).reshape(1, LANES)
    cw31 = _pad_rows(conv_w[0], CONV_HALO)
    wo = w_out[0].astype(BF16)
    pw = pool_w[0].astype(BF16)
    wup = [w_up[i].astype(BF16) for i in range(2)]
    wd = [w_down[i].astype(BF16) for i in range(2)]
    cw3 = [_pad_rows(ffn_conv_w[i], 8) for i in range(2)]

    hm = jnp.pad(meta_tokens.astype(F32), ((0, TM - N_META), (0, 0)))[None]
    hr = x.astype(F32)

    zero_c = jnp.zeros((1, LANES), F32)
    zero_ffn = jnp.zeros((FFN_HALO, 2 * D_FF), F32)

    g0 = _row(mix_norm_even[0])
    qm, km, vm, um, ccol_m, crow_m = _mix_in(hm, g0, w_in_r, bf_pad, zero_c, T=TM)
    c0 = ccol_m[0, N_META - 1:N_META, :]
    q, k, v, u, ccol, crow = _mix_in(hr, g0, w_in_r, bf_pad, c0, T=T_MAIN)
    attn_m = _attention(qm, km, vm, ccol_m, crow_m, TQ=TM)
    attn = _attention(q, k, v, ccol, crow, meta=(km, vm, crow_m), TQ=T_ATTN)
    mo_args = (cw31, _row(conv_b[0]), _row(ln_g[0]), _row(ln_b[0]), wo)
    hm = _mix_out(attn_m, um, hm, jnp.zeros((CONV_HALO, CONV_CH), F32), *mo_args, T=TM)
    hist_u = jnp.concatenate(
        [jnp.zeros((CONV_HALO - N_META, CONV_CH), F32), um[0, 0:N_META]], axis=0)
    hr = _mix_out(attn, u, hr, hist_u, *mo_args, T=T_MAIN)

    f_args = (_row(ffn_norm[0]), wup[0], cw3[0], _row(ffn_conv_b[0]), wd[0])
    hm, tail = _ffn(hm, *f_args, zero_ffn, T=TM, n_valid=N_META)
    hr, _ = _ffn(hr, *f_args, tail, T=T_MAIN, n_valid=T_MAIN)

    p_args = (_row(mix_norm_odd[0]), )
    p_tail = (pw, pool_b[0].reshape(1, -1).astype(F32), _row(pool_scale[0]))
    hm, ptail = _pool_mixer(hm, *p_args, jnp.zeros((POOL_HALO, D_MODEL), F32), *p_tail,
                            T=TM, pos0=0, n_valid=N_META)
    hr, _ = _pool_mixer(hr, *p_args, ptail, *p_tail, T=T_MAIN, pos0=N_META, n_valid=T_MAIN)

    f_args = (_row(ffn_norm[1]), wup[1], cw3[1], _row(ffn_conv_b[1]), wd[1])
    _, tail = _ffn(hm, *f_args, zero_ffn, T=TM, n_valid=N_META)
    out, _ = _ffn(hr, *f_args, tail, _row(final_norm), T=T_MAIN, n_valid=T_MAIN)
    return out.astype(x.dtype)
```

```python
import functools
import math

import numpy as np
import jax
import jax.numpy as jnp
from jax import lax
from jax.experimental import pallas as pl
from jax.experimental.pallas import tpu as pltpu

F32 = jnp.float32
BF16 = jnp.bfloat16

D_MODEL = 1024
N_META = 16
FOX_HEADS = 8
FOX_HEAD_DIM = 64
FOX_WIDTH = FOX_HEADS * FOX_HEAD_DIM
CONV_CH = D_MODEL - FOX_WIDTH
CONV_WIDTH = 31
POOL_WINDOWS = (2, 4, 8, 16)
POOL_GROUP = D_MODEL // len(POOL_WINDOWS)
D_FF = 2816
FFN_CONV_WIDTH = 3
RMS_EPS = 1e-6
LN_EPS = 1e-5

LANES = 128
SUBLANES = 8
MXU_DIM = 256
VMEM_LIMIT = 56 * 1024 * 1024

T_MAIN = 512
T_ATTN = 256
META_PAD = 128
QKVAG_COLS = 5 * FOX_WIDTH
CONV_HALO = 32
POOL_HALO = 16
FFN_HALO = 8
PARTS = 3
AUG_ONES = FOX_HEADS * PARTS
NEG_BIG = -1e30
LOG2E = math.log2(math.e)
FFN_CHUNKS = ((0, 6 * MXU_DIM), (6 * MXU_DIM, D_FF - 6 * MXU_DIM))


def _rms(x, g):
    ms = jnp.mean(x * x, axis=-1, keepdims=True)
    return x * lax.rsqrt(ms + RMS_EPS) * g


def _const_spec(shape, single_buffer=False):
    nd = len(shape)
    kw = {"pipeline_mode": pl.Buffered(1)} if single_buffer else {}
    return pl.BlockSpec(shape, lambda *_: (0,) * nd, **kw)


def _params(n_grid):
    return pltpu.CompilerParams(
        dimension_semantics=("arbitrary",) * n_grid, vmem_limit_bytes=VMEM_LIMIT)


def _split3(x):
    hi = x.astype(BF16)
    r1 = x - hi.astype(F32)
    mid = r1.astype(BF16)
    lo = (r1 - mid.astype(F32)).astype(BF16)
    return hi, mid, lo


def _shift_rows(z, k):
    return pltpu.roll(z, k, axis=0) if k else z


def _mixin_kernel(h_ref, g_ref, w_ref, wf_ref, bf_ref, c0_ref, place_ref, ones_ref,
                  q_ref, k_ref, v_ref, u_ref, caq_ref, cak_ref, clast_ref, carry, *, T, n_valid):
    t = pl.program_id(1)

    @pl.when(t == 0)
    def _():
        carry[...] = c0_ref[...]

    hn = _rms(h_ref[0], g_ref[...]).astype(BF16)
    W = FOX_WIDTH
    fl = jnp.dot(hn, wf_ref[...], preferred_element_type=F32) + bf_ref[...]
    logf = jnp.minimum(fl, 0.0) - jnp.log1p(jnp.exp(-jnp.abs(fl)))
    lane = lax.broadcasted_iota(jnp.int32, (T, LANES), 1)
    logf_pieces = _split3(jnp.where(lane < FOX_HEADS, logf, 0.0))
    row = lax.broadcasted_iota(jnp.int32, (T, T), 0)
    col = lax.broadcasted_iota(jnp.int32, (T, T), 1)
    tri = jnp.where(row >= col, 1.0, 0.0).astype(BF16)

    p = jnp.dot(hn, w_ref[:, 0:3 * W], preferred_element_type=F32)
    q_ref[0] = (p[:, 0:W] * (FOX_HEAD_DIM ** -0.5 * LOG2E)).astype(BF16)
    k_ref[0] = p[:, W:2 * W].astype(BF16)
    v_ref[0] = p[:, 2 * W:3 * W].astype(BF16)

    c = carry[...]
    for piece in logf_pieces:
        c = c + jnp.dot(tri, piece, preferred_element_type=F32)
    carry[...] = c[T - 1:T, :]
    clast_ref[...] = c[n_valid - 1:n_valid, :]
    c_pieces = jnp.concatenate(_split3(c * LOG2E), axis=1)

    p = jnp.dot(hn, w_ref[:, 3 * W:5 * W], preferred_element_type=F32)
    u_ref[0] = p[:, 0:W] * jax.nn.sigmoid(p[:, W:2 * W])

    aug = jnp.dot(c_pieces, place_ref[...], preferred_element_type=F32) + ones_ref[...]
    caq_ref[0] = aug[:, 0:LANES].astype(BF16)
    cak_ref[0] = aug[:, LANES:2 * LANES].astype(BF16)


def _mix_in(h, g, w5, wf, bf_pad, c0, place, ones, *, T, n_valid):
    nb, L, D = h.shape
    tile = lambda w: pl.BlockSpec((1, T, w), lambda b, t: (b, t, 0))
    return pl.pallas_call(
        functools.partial(_mixin_kernel, T=T, n_valid=n_valid),
        grid=(nb, L // T),
        in_specs=[tile(D), _const_spec((1, D)), _const_spec((D, QKVAG_COLS)),
                  _const_spec((D, LANES)), _const_spec((1, LANES)), _const_spec((1, LANES)),
                  _const_spec((PARTS * LANES, 2 * LANES)), _const_spec((1, 2 * LANES))],
        out_specs=[tile(FOX_WIDTH), tile(FOX_WIDTH), tile(FOX_WIDTH), tile(CONV_CH),
                   tile(LANES), tile(LANES), _const_spec((1, LANES))],
        out_shape=[jax.ShapeDtypeStruct((nb, L, FOX_WIDTH), BF16)] * 3
        + [jax.ShapeDtypeStruct((nb, L, CONV_CH), F32),
           jax.ShapeDtypeStruct((nb, L, LANES), BF16),
           jax.ShapeDtypeStruct((nb, L, LANES), BF16),
           jax.ShapeDtypeStruct((1, LANES), F32)],
        scratch_shapes=[pltpu.VMEM((1, LANES), F32)],
        compiler_params=_params(2),
        name="mix_in",
    )(h, g, w5, wf, bf_pad, c0, place, ones)


def _attn_kernel(*refs, L, TQ, n_meta):
    if n_meta:
        q_ref, caq_ref, k_ref, cak_ref, v_ref, km_ref, cakm_ref, vm_ref, o_ref = refs
    else:
        q_ref, caq_ref, k_ref, cak_ref, v_ref, o_ref = refs
    hp = pl.program_id(1)
    lane = lax.broadcasted_iota(jnp.int32, (TQ, LANES), 1)
    first = lane < FOX_HEAD_DIM
    row = lax.broadcasted_iota(jnp.int32, (2 * TQ, TQ), 0)
    col = lax.broadcasted_iota(jnp.int32, (2 * TQ, TQ), 1)
    causal = (row % TQ) >= col
    meta_mask = lax.broadcasted_iota(jnp.int32, (2 * TQ, LANES), 1) < n_meta
    dn_nt = (((1,), (1,)), ((), ()))

    def scores(qa, k_rows, cak_rows, mask):
        ka = jnp.concatenate([k_rows, cak_rows], axis=1)
        s = lax.dot_general(qa, ka, dn_nt, preferred_element_type=F32)
        return s if mask is None else jnp.where(mask, s, NEG_BIG)

    def head_lanes(h):
        lo = PARTS * (2 * hp + h)
        return (((lane >= lo) & (lane < lo + PARTS))
                | ((lane >= lo + AUG_ONES) & (lane < lo + AUG_ONES + PARTS)))

    for qi in reversed(range(L // TQ)):
        r0 = qi * TQ
        q2 = q_ref[0, r0:r0 + TQ, :]
        caq = caq_ref[0, r0:r0 + TQ, :]
        zero = jnp.zeros_like(q2)
        qa = jnp.concatenate(
            [jnp.concatenate([jnp.where(first, q2, zero), jnp.where(head_lanes(0), caq, zero)], 1),
             jnp.concatenate([jnp.where(first, zero, q2), jnp.where(head_lanes(1), caq, zero)], 1)],
            axis=0)
        parts = [(scores(qa, k_ref[0, r0:r0 + TQ, :], cak_ref[0, r0:r0 + TQ, :], causal),
                  v_ref[0, r0:r0 + TQ, :])]
        if qi:
            parts.append((scores(qa, k_ref[0, 0:r0, :], cak_ref[0, 0:r0, :], None),
                          v_ref[0, 0:r0, :]))
        if n_meta:
            parts.append((scores(qa, km_ref[0], cakm_ref[0], meta_mask), vm_ref[0]))
        m = functools.reduce(jnp.maximum, [jnp.max(s, axis=1, keepdims=True) for s, _ in parts])
        l = 0.0
        acc = 0.0
        for s, vt in parts:
            p = jnp.exp2(s - m)
            l = l + jnp.sum(p, axis=1, keepdims=True)
            acc = acc + jnp.dot(p.astype(BF16), vt, preferred_element_type=F32)
        o = acc / l
        o_ref[0, r0:r0 + TQ, :] = jnp.where(first, o[0:TQ], o[TQ:2 * TQ]).astype(BF16)


def _attention(q, caq, k, cak, v, meta=None, *, TQ):
    nb, L, _ = q.shape
    n_hp = FOX_WIDTH // LANES
    seq = pl.BlockSpec((1, L, LANES), lambda b, hp: (b, 0, hp))
    aug = pl.BlockSpec((1, L, LANES), lambda b, hp: (b, 0, 0))
    in_specs = [seq, aug, seq, aug, seq]
    args = [q, caq, k, cak, v]
    if meta is not None:
        mseq = pl.BlockSpec((1, LANES, LANES), lambda b, hp: (0, 0, hp))
        in_specs += [mseq, pl.BlockSpec((1, LANES, LANES), lambda b, hp: (0, 0, 0)), mseq]
        args += list(meta)
    return pl.pallas_call(
        functools.partial(_attn_kernel, L=L, TQ=TQ, n_meta=N_META if meta is not None else 0),
        grid=(nb, n_hp),
        in_specs=in_specs,
        out_specs=seq,
        out_shape=jax.ShapeDtypeStruct((nb, L, FOX_WIDTH), BF16),
        compiler_params=_params(2),
        name="fox_attention",
    )(*args)


def _mixout_kernel(attn_ref, u_ref, h_ref, hist_ref, cw_ref, cb_ref, lg_ref, lb_ref, wo_ref,
                   out_ref, ubuf, *, T):
    t = pl.program_id(1)

    @pl.when(t == 0)
    def _():
        ubuf[0:CONV_HALO, :] = hist_ref[...]

    H = CONV_HALO
    ubuf[H:H + T, :] = u_ref[0]
    S = SUBLANES
    acc = None
    for rho in range(S):
        z = None
        for a in range((CONV_WIDTH - 1 - rho) // S + 1):
            kk = CONV_WIDTH - 1 - (S * a + rho)
            start = H - S - S * a
            term = cw_ref[kk:kk + 1, :] * ubuf[start:start + T + S, :]
            z = term if z is None else z + term
        z = _shift_rows(z, rho)
        acc = z if acc is None else acc + z
    conv = acc[S:S + T, :] + cb_ref[...]
    ubuf[0:H, :] = ubuf[T:T + H, :]
    mu = jnp.mean(conv, axis=-1, keepdims=True)
    d = conv - mu
    var = jnp.mean(d * d, axis=-1, keepdims=True)
    y = d * lax.rsqrt(var + LN_EPS) * lg_ref[...] + lb_ref[...]
    z = y * jax.nn.sigmoid(y)
    cat = jnp.concatenate([attn_ref[0], z.astype(BF16)], axis=1)
    out_ref[0] = h_ref[0] + jnp.dot(cat, wo_ref[...], preferred_element_type=F32)


def _mix_out(attn, u, h, hist, cw, cb, lg, lb, wo, *, T):
    nb, L, D = h.shape
    tile = lambda w: pl.BlockSpec((1, T, w), lambda b, t: (b, t, 0))
    return pl.pallas_call(
        functools.partial(_mixout_kernel, T=T),
        grid=(nb, L // T),
        in_specs=[tile(FOX_WIDTH), tile(CONV_CH), tile(D), _const_spec((CONV_HALO, CONV_CH)),
                  _const_spec((CONV_HALO, CONV_CH)), _const_spec((1, CONV_CH)),
                  _const_spec((1, CONV_CH)), _const_spec((1, CONV_CH)), _const_spec((D, D))],
        out_specs=tile(D),
        out_shape=jax.ShapeDtypeStruct((nb, L, D), F32),
        scratch_shapes=[pltpu.VMEM((CONV_HALO + T, CONV_CH), F32)],
        compiler_params=_params(2),
        name="mix_out",
    )(attn, u, h, hist, cw, cb, lg, lb, wo)


def _pool_kernel(h_ref, g_ref, hist_ref, pw_ref, pb_ref, ps_ref, out_ref, tail_ref, buf,
                 *, T, pos0, n_valid):
    t = pl.program_id(1)

    @pl.when(t == 0)
    def _():
        buf[0:POOL_HALO, :] = hist_ref[...]

    H = POOL_HALO
    x = h_ref[0]
    buf[H:H + T, :] = _rms(x, g_ref[...])
    n_seen = lax.broadcasted_iota(jnp.int32, (T, 1), 0) + (t * T + pos0 + 1)
    G = POOL_GROUP
    for gi, w in enumerate(POOL_WINDOWS):
        cs = slice(gi * G, (gi + 1) * G)
        s = buf[0:H + T, cs]
        k = 1
        while k < w:
            s = s + _shift_rows(s, k)
            k *= 2
        inv_cnt = 1.0 / jnp.minimum(n_seen, w).astype(F32)
        d = s[H:H + T, :] * inv_cnt - buf[H:H + T, cs]
        y = jnp.dot(d.astype(BF16), pw_ref[gi], preferred_element_type=F32) + pb_ref[:, cs]
        out_ref[0, :, cs] = x[:, cs] + y * ps_ref[:, cs]
    tail_ref[...] = buf[n_valid:n_valid + H, :]
    buf[0:H, :] = buf[T:T + H, :]


def _pool_mixer(h, g, hist, pw, pb, ps, *, T, pos0, n_valid):
    nb, L, D = h.shape
    tile = pl.BlockSpec((1, T, D), lambda b, t: (b, t, 0))
    return pl.pallas_call(
        functools.partial(_pool_kernel, T=T, pos0=pos0, n_valid=n_valid),
        grid=(nb, L // T),
        in_specs=[tile, _const_spec((1, D)), _const_spec((POOL_HALO, D)),
                  _const_spec((len(POOL_WINDOWS), POOL_GROUP, POOL_GROUP)),
                  _const_spec((1, D)), _const_spec((1, D))],
        out_specs=[tile, _const_spec((POOL_HALO, D))],
        out_shape=[jax.ShapeDtypeStruct((nb, L, D), F32),
                   jax.ShapeDtypeStruct((POOL_HALO, D), F32)],
        scratch_shapes=[pltpu.VMEM((POOL_HALO + T, D), F32)],
        compiler_params=_params(2),
        name="pool_mixer",
    )(h, g, hist, pw, pb, ps)


def _ffn_kernel(*refs, T, n_valid, final_norm):
    n_in = 8 if final_norm else 7
    h_ref, g_ref, wup_ref, cw_ref, cb_ref, wd_ref, hist_ref = refs[:7]
    fn_ref = refs[7] if final_norm else None
    out_ref, tail_ref, halo = refs[n_in:n_in + 3]
    bufs = refs[n_in + 3:]
    t = pl.program_id(1)

    @pl.when(t == 0)
    def _():
        halo[...] = hist_ref[...]

    x = h_ref[0]
    hn = _rms(x, g_ref[...]).astype(BF16)
    H = FFN_HALO

    def up_project(c):
        c0, fc = FFN_CHUNKS[c]
        for half, col0 in enumerate((c0, D_FF + c0)):
            cs = slice(col0, col0 + fc)
            buf = bufs[2 * c + half]
            buf[0:H, :] = halo[:, cs]
            buf[H:H + T, :] = jnp.dot(hn, wup_ref[:, cs], preferred_element_type=F32)
            halo[:, cs] = buf[T:T + H, :]
            tail_ref[:, cs] = buf[n_valid:n_valid + H, :]

    def activation(c):
        c0, fc = FFN_CHUNKS[c]
        halves = []
        for half, col0 in enumerate((c0, D_FF + c0)):
            cs = slice(col0, col0 + fc)
            ext = bufs[2 * c + half][...]
            conv = cw_ref[FFN_CONV_WIDTH - 1:FFN_CONV_WIDTH, cs] * ext
            for d in range(1, FFN_CONV_WIDTH):
                kk = FFN_CONV_WIDTH - 1 - d
                conv = conv + cw_ref[kk:kk + 1, cs] * _shift_rows(ext, d)
            halves.append(conv[H:H + T, :] + cb_ref[:, cs])
        gate, val = halves
        return (gate * jax.nn.sigmoid(gate) * val).astype(BF16)

    y = x
    up_project(0)
    for c, (c0, fc) in enumerate(FFN_CHUNKS):
        if c + 1 < len(FFN_CHUNKS):
            up_project(c + 1)
        y = y + jnp.dot(activation(c), wd_ref[c0:c0 + fc, :], preferred_element_type=F32)
    if final_norm:
        y = _rms(y, fn_ref[...])
    out_ref[0] = y


def _ffn(h, g, wup, cw, cb, wd, hist, fn=None, *, T, n_valid):
    nb, L, D = h.shape
    tile = pl.BlockSpec((1, T, D), lambda b, t: (b, t, 0))
    in_specs = [tile, _const_spec((1, D)), _const_spec((D, 2 * D_FF), True),
                _const_spec((SUBLANES, 2 * D_FF)), _const_spec((1, 2 * D_FF)),
                _const_spec((D_FF, D), True), _const_spec((FFN_HALO, 2 * D_FF))]
    args = [h, g, wup, cw, cb, wd, hist]
    if fn is not None:
        in_specs.append(_const_spec((1, D)))
        args.append(fn)
    scratch = [pltpu.VMEM((FFN_HALO, 2 * D_FF), F32)]
    for _, fc in FFN_CHUNKS:
        scratch += [pltpu.VMEM((FFN_HALO + T, fc), F32)] * 2
    return pl.pallas_call(
        functools.partial(_ffn_kernel, T=T, n_valid=n_valid, final_norm=fn is not None),
        grid=(nb, L // T),
        in_specs=in_specs,
        out_specs=[tile, _const_spec((FFN_HALO, 2 * D_FF))],
        out_shape=[jax.ShapeDtypeStruct((nb, L, D), F32),
                   jax.ShapeDtypeStruct((FFN_HALO, 2 * D_FF), F32)],
        scratch_shapes=scratch,
        compiler_params=_params(2),
        name="conv_glu_ffn",
    )(*args)


def _row(v):
    return v.reshape(1, -1).astype(F32)


def _pad_rows(a, rows):
    return jnp.pad(a.astype(F32), ((0, rows - a.shape[0]), (0, 0)))


def _aug_constants():
    place = np.zeros((PARTS * LANES, 2 * LANES), np.float32)
    ones = np.zeros((1, 2 * LANES), np.float32)
    for h in range(FOX_HEADS):
        for j in range(PARTS):
            place[j * LANES + h, AUG_ONES + PARTS * h + j] = 1.0
            place[j * LANES + h, LANES + PARTS * h + j] = -1.0
            ones[0, PARTS * h + j] = 1.0
            ones[0, LANES + AUG_ONES + PARTS * h + j] = 1.0
    return jnp.asarray(place, BF16), jnp.asarray(ones, F32)


def kernel(x, meta_tokens, mix_norm_even, w_in, b_f, conv_w, conv_b, ln_g, ln_b, w_out,
           mix_norm_odd, pool_w, pool_b, pool_scale, ffn_norm, w_up, ffn_conv_w, ffn_conv_b,
           w_down, final_norm):
    W = FOX_WIDTH
    TM = META_PAD

    wi = w_in[0]
    w5 = jnp.concatenate([wi[:, 0:3 * W], wi[:, 3 * W + FOX_HEADS:]], axis=1).astype(BF16)
    wf = jnp.pad(wi[:, 3 * W:3 * W + FOX_HEADS], ((0, 0), (0, LANES - FOX_HEADS))).astype(BF16)
    bf_pad = jnp.pad(b_f[0].astype(F32), (0, LANES - FOX_HEADS)).reshape(1, LANES)
    place, ones = _aug_constants()
    cw31 = _pad_rows(conv_w[0], CONV_HALO)
    wo = w_out[0].astype(BF16)
    pw = pool_w[0].astype(BF16)
    wup = [w_up[i].astype(BF16) for i in range(2)]
    wd = [w_down[i].astype(BF16) for i in range(2)]
    cw3 = [_pad_rows(ffn_conv_w[i], SUBLANES) for i in range(2)]

    hm = jnp.pad(meta_tokens.astype(F32), ((0, TM - N_META), (0, 0)))[None]
    hr = x.astype(F32)

    zero_c = jnp.zeros((1, LANES), F32)
    zero_ffn = jnp.zeros((FFN_HALO, 2 * D_FF), F32)

    mi_args = (_row(mix_norm_even[0]), w5, wf, bf_pad)
    qm, km, vm, um, caq_m, cak_m, c_meta = _mix_in(
        hm, *mi_args, zero_c, place, ones, T=TM, n_valid=N_META)
    q, k, v, u, caq, cak, _ = _mix_in(hr, *mi_args, c_meta, place, ones, T=T_MAIN, n_valid=T_MAIN)
    attn_m = _attention(qm, caq_m, km, cak_m, vm, TQ=TM)
    attn = _attention(q, caq, k, cak, v, meta=(km, cak_m, vm), TQ=T_ATTN)
    mo_args = (cw31, _row(conv_b[0]), _row(ln_g[0]), _row(ln_b[0]), wo)
    hm = _mix_out(attn_m, um, hm, jnp.zeros((CONV_HALO, CONV_CH), F32), *mo_args, T=TM)
    hist_u = jnp.concatenate(
        [jnp.zeros((CONV_HALO - N_META, CONV_CH), F32), um[0, 0:N_META]], axis=0)
    hr = _mix_out(attn, u, hr, hist_u, *mo_args, T=T_MAIN)

    f_args = (_row(ffn_norm[0]), wup[0], cw3[0], _row(ffn_conv_b[0]), wd[0])
    hm, tail = _ffn(hm, *f_args, zero_ffn, T=TM, n_valid=N_META)
    hr, _ = _ffn(hr, *f_args, tail, T=T_MAIN, n_valid=T_MAIN)

    p_args = (pw, pool_b[0].reshape(1, -1).astype(F32), _row(pool_scale[0]))
    g1 = _row(mix_norm_odd[0])
    hm, ptail = _pool_mixer(hm, g1, jnp.zeros((POOL_HALO, D_MODEL), F32), *p_args,
                            T=TM, pos0=0, n_valid=N_META)
    hr, _ = _pool_mixer(hr, g1, ptail, *p_args, T=T_MAIN, pos0=N_META, n_valid=T_MAIN)

    f_args = (_row(ffn_norm[1]), wup[1], cw3[1], _row(ffn_conv_b[1]), wd[1])
    _, tail = _ffn(hm, *f_args, zero_ffn, T=TM, n_valid=N_META)
    out, _ = _ffn(hr, *f_args, tail, _row(final_norm), T=T_MAIN, n_valid=T_MAIN)
    return out.astype(x.dtype)
```

```python
import functools
import math

import numpy as np
import jax
import jax.numpy as jnp
from jax import lax
from jax.experimental import pallas as pl
from jax.experimental.pallas import tpu as pltpu

F32 = jnp.float32
BF16 = jnp.bfloat16

D_MODEL = 1024
N_META = 16
FOX_HEADS = 8
FOX_HEAD_DIM = 64
FOX_WIDTH = FOX_HEADS * FOX_HEAD_DIM
CONV_CH = D_MODEL - FOX_WIDTH
CONV_WIDTH = 31
POOL_WINDOWS = (2, 4, 8, 16)
POOL_GROUP = D_MODEL // len(POOL_WINDOWS)
D_FF = 2816
FFN_CONV_WIDTH = 3
RMS_EPS = 1e-6
LN_EPS = 1e-5

LANES = 128
SUBLANES = 8
MXU_DIM = 256
VMEM_LIMIT = 56 * 1024 * 1024

T_MAIN = 512
T_ATTN = 256
META_PAD = 128
QKVAG_COLS = 5 * FOX_WIDTH
CONV_HALO = 32
POOL_HALO = 16
FFN_HALO = 8
PARTS = 3
AUG_ONES = FOX_HEADS * PARTS
NEG_BIG = -1e30
LOG2E = math.log2(math.e)
FFN_CHUNKS = ((0, 6 * MXU_DIM), (6 * MXU_DIM, D_FF - 6 * MXU_DIM))


def _rms(x, g):
    ms = jnp.mean(x * x, axis=-1, keepdims=True)
    return x * lax.rsqrt(ms + RMS_EPS) * g


def _const_spec(shape, single_buffer=False):
    nd = len(shape)
    kw = {"pipeline_mode": pl.Buffered(1)} if single_buffer else {}
    return pl.BlockSpec(shape, lambda *_: (0,) * nd, **kw)


def _layer_spec(shape, layer, single_buffer=False):
    nd = len(shape)
    kw = {"pipeline_mode": pl.Buffered(1)} if single_buffer else {}
    return pl.BlockSpec((None,) + tuple(shape), lambda *_: (layer,) + (0,) * nd, **kw)


def _params(n_grid):
    return pltpu.CompilerParams(
        dimension_semantics=("arbitrary",) * n_grid, vmem_limit_bytes=VMEM_LIMIT)


def _split3(x):
    hi = x.astype(BF16)
    r1 = x - hi.astype(F32)
    mid = r1.astype(BF16)
    lo = (r1 - mid.astype(F32)).astype(BF16)
    return hi, mid, lo


def _shift_rows(z, k):
    return pltpu.roll(z, k, axis=0) if k else z


def _conv_module(u, ubuf, tail_ref, cw_ref, cb_ref, lg_ref, lb_ref, T, n_valid):
    H = CONV_HALO
    S = SUBLANES
    ubuf[H:H + T, :] = u
    tail_ref[...] = ubuf[n_valid:n_valid + H, :]
    acc = None
    for rho in range(S):
        z = None
        for a in range((CONV_WIDTH - 1 - rho) // S + 1):
            kk = CONV_WIDTH - 1 - (S * a + rho)
            start = H - S - S * a
            term = cw_ref[kk:kk + 1, :] * ubuf[start:start + T + S, :]
            z = term if z is None else z + term
        z = _shift_rows(z, rho)
        acc = z if acc is None else acc + z
    conv = acc[S:S + T, :] + cb_ref[...]
    ubuf[0:H, :] = ubuf[T:T + H, :]
    mu = jnp.mean(conv, axis=-1, keepdims=True)
    d = conv - mu
    var = jnp.mean(d * d, axis=-1, keepdims=True)
    y = d * lax.rsqrt(var + LN_EPS) * lg_ref[...] + lb_ref[...]
    return y * jax.nn.sigmoid(y)


def _mixin_kernel(h_ref, g_ref, w_ref, wf_ref, bf_ref, c0_ref, place_ref, ones_ref,
                  uhist_ref, cw_ref, cb_ref, lg_ref, lb_ref,
                  q_ref, k_ref, v_ref, z_ref, caq_ref, cak_ref, clast_ref, utail_ref,
                  carry, ubuf, *, T, n_valid):
    t = pl.program_id(1)

    @pl.when(t == 0)
    def _():
        carry[...] = c0_ref[...]
        ubuf[0:CONV_HALO, :] = uhist_ref[...]

    hn = _rms(h_ref[0], g_ref[...]).astype(BF16)
    W = FOX_WIDTH
    fl = jnp.dot(hn, wf_ref[...], preferred_element_type=F32) + bf_ref[...]
    logf = jnp.minimum(fl, 0.0) - jnp.log1p(jnp.exp(-jnp.abs(fl)))
    lane = lax.broadcasted_iota(jnp.int32, (T, LANES), 1)
    logf_pieces = _split3(jnp.where(lane < FOX_HEADS, logf, 0.0))
    row = lax.broadcasted_iota(jnp.int32, (T, T), 0)
    col = lax.broadcasted_iota(jnp.int32, (T, T), 1)
    tri = jnp.where(row >= col, 1.0, 0.0).astype(BF16)

    p = jnp.dot(hn, w_ref[:, 3 * W:5 * W], preferred_element_type=F32)
    u = p[:, 0:W] * jax.nn.sigmoid(p[:, W:2 * W])
    z_ref[0] = _conv_module(u, ubuf, utail_ref, cw_ref, cb_ref, lg_ref, lb_ref,
                            T, n_valid).astype(BF16)

    p = jnp.dot(hn, w_ref[:, 0:3 * W], preferred_element_type=F32)
    q_ref[0] = (p[:, 0:W] * (FOX_HEAD_DIM ** -0.5 * LOG2E)).astype(BF16)
    k_ref[0] = p[:, W:2 * W].astype(BF16)
    v_ref[0] = p[:, 2 * W:3 * W].astype(BF16)

    c = carry[...]
    for piece in logf_pieces:
        c = c + jnp.dot(tri, piece, preferred_element_type=F32)
    carry[...] = c[T - 1:T, :]
    clast_ref[...] = c[n_valid - 1:n_valid, :]
    c_pieces = jnp.concatenate(_split3(c * LOG2E), axis=1)
    aug = jnp.dot(c_pieces, place_ref[...], preferred_element_type=F32) + ones_ref[...]
    caq_ref[0] = aug[:, 0:LANES].astype(BF16)
    cak_ref[0] = aug[:, LANES:2 * LANES].astype(BF16)


def _mix_in(h, g, w5, wf, bf_pad, c0, place, ones, uhist, cw, cb, lg, lb, *, T, n_valid):
    nb, L, D = h.shape
    tile = lambda w: pl.BlockSpec((1, T, w), lambda b, t: (b, t, 0))
    return pl.pallas_call(
        functools.partial(_mixin_kernel, T=T, n_valid=n_valid),
        grid=(nb, L // T),
        in_specs=[tile(D), _const_spec((1, D)), _const_spec((D, QKVAG_COLS)),
                  _const_spec((D, LANES)), _const_spec((1, LANES)), _const_spec((1, LANES)),
                  _const_spec((PARTS * LANES, 2 * LANES)), _const_spec((1, 2 * LANES)),
                  _const_spec((CONV_HALO, CONV_CH)), _const_spec((CONV_HALO, CONV_CH)),
                  _const_spec((1, CONV_CH)), _const_spec((1, CONV_CH)), _const_spec((1, CONV_CH))],
        out_specs=[tile(FOX_WIDTH), tile(FOX_WIDTH), tile(FOX_WIDTH), tile(CONV_CH),
                   tile(LANES), tile(LANES), _const_spec((1, LANES)),
                   _const_spec((CONV_HALO, CONV_CH))],
        out_shape=[jax.ShapeDtypeStruct((nb, L, FOX_WIDTH), BF16)] * 3
        + [jax.ShapeDtypeStruct((nb, L, CONV_CH), BF16),
           jax.ShapeDtypeStruct((nb, L, LANES), BF16),
           jax.ShapeDtypeStruct((nb, L, LANES), BF16),
           jax.ShapeDtypeStruct((1, LANES), F32),
           jax.ShapeDtypeStruct((CONV_HALO, CONV_CH), F32)],
        scratch_shapes=[pltpu.VMEM((1, LANES), F32), pltpu.VMEM((CONV_HALO + T, CONV_CH), F32)],
        compiler_params=_params(2),
        name="mix_in",
    )(h, g, w5, wf, bf_pad, c0, place, ones, uhist, cw, cb, lg, lb)


def _attn_kernel(*refs, L, TQ, n_meta):
    if n_meta:
        q_ref, caq_ref, k_ref, cak_ref, v_ref, km_ref, cakm_ref, vm_ref, o_ref = refs
    else:
        q_ref, caq_ref, k_ref, cak_ref, v_ref, o_ref = refs
    hp = pl.program_id(1)
    lane = lax.broadcasted_iota(jnp.int32, (TQ, LANES), 1)
    first = lane < FOX_HEAD_DIM
    row = lax.broadcasted_iota(jnp.int32, (2 * TQ, TQ), 0)
    col = lax.broadcasted_iota(jnp.int32, (2 * TQ, TQ), 1)
    causal = (row % TQ) >= col
    meta_mask = lax.broadcasted_iota(jnp.int32, (2 * TQ, LANES), 1) < n_meta
    dn_nt = (((1,), (1,)), ((), ()))

    def scores(qa, k_rows, cak_rows, mask):
        ka = jnp.concatenate([k_rows, cak_rows], axis=1)
        s = lax.dot_general(qa, ka, dn_nt, preferred_element_type=F32)
        return s if mask is None else jnp.where(mask, s, NEG_BIG)

    def head_lanes(h):
        lo = PARTS * (2 * hp + h)
        return (((lane >= lo) & (lane < lo + PARTS))
                | ((lane >= lo + AUG_ONES) & (lane < lo + AUG_ONES + PARTS)))

    for qi in reversed(range(L // TQ)):
        r0 = qi * TQ
        q2 = q_ref[0, r0:r0 + TQ, :]
        caq = caq_ref[0, r0:r0 + TQ, :]
        zero = jnp.zeros_like(q2)
        qa = jnp.concatenate(
            [jnp.concatenate([jnp.where(first, q2, zero), jnp.where(head_lanes(0), caq, zero)], 1),
             jnp.concatenate([jnp.where(first, zero, q2), jnp.where(head_lanes(1), caq, zero)], 1)],
            axis=0)
        parts = [(scores(qa, k_ref[0, r0:r0 + TQ, :], cak_ref[0, r0:r0 + TQ, :], causal),
                  v_ref[0, r0:r0 + TQ, :])]
        if qi:
            parts.append((scores(qa, k_ref[0, 0:r0, :], cak_ref[0, 0:r0, :], None),
                          v_ref[0, 0:r0, :]))
        if n_meta:
            parts.append((scores(qa, km_ref[0], cakm_ref[0], meta_mask), vm_ref[0]))
        m = functools.reduce(jnp.maximum, [jnp.max(s, axis=1, keepdims=True) for s, _ in parts])
        l = 0.0
        acc = 0.0
        for s, vt in parts:
            p = jnp.exp2(s - m)
            l = l + jnp.sum(p, axis=1, keepdims=True)
            acc = acc + jnp.dot(p.astype(BF16), vt, preferred_element_type=F32)
        o = acc / l
        o_ref[0, r0:r0 + TQ, :] = jnp.where(first, o[0:TQ], o[TQ:2 * TQ]).astype(BF16)


def _attention(q, caq, k, cak, v, meta=None, *, TQ):
    nb, L, _ = q.shape
    n_hp = FOX_WIDTH // LANES
    seq = pl.BlockSpec((1, L, LANES), lambda b, hp: (b, 0, hp))
    aug = pl.BlockSpec((1, L, LANES), lambda b, hp: (b, 0, 0))
    in_specs = [seq, aug, seq, aug, seq]
    args = [q, caq, k, cak, v]
    if meta is not None:
        mseq = pl.BlockSpec((1, LANES, LANES), lambda b, hp: (0, 0, hp))
        in_specs += [mseq, pl.BlockSpec((1, LANES, LANES), lambda b, hp: (0, 0, 0)), mseq]
        args += list(meta)
    return pl.pallas_call(
        functools.partial(_attn_kernel, L=L, TQ=TQ, n_meta=N_META if meta is not None else 0),
        grid=(nb, n_hp),
        in_specs=in_specs,
        out_specs=seq,
        out_shape=jax.ShapeDtypeStruct((nb, L, FOX_WIDTH), BF16),
        compiler_params=_params(2),
        name="fox_attention",
    )(*args)


def _pool_stage(x, t, g_ref, hist_ref, pw_ref, pb_ref, ps_ref, tail_ref, buf, T, pos0, n_valid):
    H = POOL_HALO

    @pl.when(t == 0)
    def _():
        buf[0:H, :] = hist_ref[...]

    buf[H:H + T, :] = _rms(x, g_ref[...])
    tail_ref[...] = buf[n_valid:n_valid + H, :]
    n_seen = lax.broadcasted_iota(jnp.int32, (T, 1), 0) + (t * T + pos0 + 1)
    G = POOL_GROUP
    outs = []
    for gi, w in enumerate(POOL_WINDOWS):
        cs = slice(gi * G, (gi + 1) * G)
        s = buf[0:H + T, cs]
        k = 1
        while k < w:
            s = s + _shift_rows(s, k)
            k *= 2
        inv_cnt = 1.0 / jnp.minimum(n_seen, w).astype(F32)
        d = s[H:H + T, :] * inv_cnt - buf[H:H + T, cs]
        y = jnp.dot(d.astype(BF16), pw_ref[gi], preferred_element_type=F32) + pb_ref[:, cs]
        outs.append(x[:, cs] + y * ps_ref[:, cs])
    buf[0:H, :] = buf[T:T + H, :]
    return jnp.concatenate(outs, axis=1)


def _ffn_kernel(*refs, T, n_valid, pos0, mixer, final_norm):
    it = iter(refs)
    h_ref = next(it)
    if mixer == "proj":
        attn_ref, z_ref, wo_ref = next(it), next(it), next(it)
    else:
        pg_ref, phist_ref, pw_ref, pb_ref, ps_ref = (next(it) for _ in range(5))
    g_ref, wup_ref, cw_ref, cb_ref, wd_ref, hist_ref = (next(it) for _ in range(6))
    fn_ref = next(it) if final_norm else None
    out_ref, tail_ref = next(it), next(it)
    ptail_ref = next(it) if mixer == "pool" else None
    halo = next(it)
    pbuf = next(it) if mixer == "pool" else None
    bufs = list(it)
    t = pl.program_id(1)

    @pl.when(t == 0)
    def _():
        halo[...] = hist_ref[...]

    if mixer == "proj":
        cat = jnp.concatenate([attn_ref[0], z_ref[0]], axis=1)
        x = h_ref[0] + jnp.dot(cat, wo_ref[...], preferred_element_type=F32)
    else:
        x = _pool_stage(h_ref[0], t, pg_ref, phist_ref, pw_ref, pb_ref, ps_ref, ptail_ref, pbuf,
                        T, pos0, n_valid)
    hn = _rms(x, g_ref[...]).astype(BF16)
    H = FFN_HALO

    def up_project(c):
        c0, fc = FFN_CHUNKS[c]
        for half, col0 in enumerate((c0, D_FF + c0)):
            cs = slice(col0, col0 + fc)
            buf = bufs[2 * c + half]
            buf[0:H, :] = halo[:, cs]
            buf[H:H + T, :] = jnp.dot(hn, wup_ref[:, cs], preferred_element_type=F32)
            halo[:, cs] = buf[T:T + H, :]
            tail_ref[:, cs] = buf[n_valid:n_valid + H, :]

    def activation(c):
        c0, fc = FFN_CHUNKS[c]
        halves = []
        for half, col0 in enumerate((c0, D_FF + c0)):
            cs = slice(col0, col0 + fc)
            ext = bufs[2 * c + half][...]
            conv = cw_ref[FFN_CONV_WIDTH - 1:FFN_CONV_WIDTH, cs] * ext
            for d in range(1, FFN_CONV_WIDTH):
                kk = FFN_CONV_WIDTH - 1 - d
                conv = conv + cw_ref[kk:kk + 1, cs] * _shift_rows(ext, d)
            halves.append(conv[H:H + T, :] + cb_ref[:, cs])
        gate, val = halves
        return (gate * jax.nn.sigmoid(gate) * val).astype(BF16)

    y = x
    up_project(0)
    for c, (c0, fc) in enumerate(FFN_CHUNKS):
        if c + 1 < len(FFN_CHUNKS):
            up_project(c + 1)
        y = y + jnp.dot(activation(c), wd_ref[c0:c0 + fc, :], preferred_element_type=F32)
    if final_norm:
        y = _rms(y, fn_ref[...])
    out_ref[0] = y


def _ffn(h, mixer, mixer_args, layer, g, wup, cw, cb, wd, hist, fn=None, *, T, n_valid, pos0):
    nb, L, D = h.shape
    tile = lambda w: pl.BlockSpec((1, T, w), lambda b, t: (b, t, 0))
    if mixer == "proj":
        mixer_specs = [tile(FOX_WIDTH), tile(CONV_CH), _const_spec((D, D))]
    else:
        mixer_specs = [_const_spec((1, D)), _const_spec((POOL_HALO, D)),
                       _const_spec((len(POOL_WINDOWS), POOL_GROUP, POOL_GROUP)),
                       _const_spec((1, D)), _const_spec((1, D))]
    in_specs = [tile(D)] + mixer_specs + [
        _const_spec((1, D)), _layer_spec((D, 2 * D_FF), layer, True),
        _const_spec((SUBLANES, 2 * D_FF)), _const_spec((1, 2 * D_FF)),
        _layer_spec((D_FF, D), layer, True), _const_spec((FFN_HALO, 2 * D_FF))]
    args = [h, *mixer_args, g, wup, cw, cb, wd, hist]
    if fn is not None:
        in_specs.append(_const_spec((1, D)))
        args.append(fn)
    out_specs = [tile(D), _const_spec((FFN_HALO, 2 * D_FF))]
    out_shape = [jax.ShapeDtypeStruct((nb, L, D), F32),
                 jax.ShapeDtypeStruct((FFN_HALO, 2 * D_FF), F32)]
    scratch = [pltpu.VMEM((FFN_HALO, 2 * D_FF), F32)]
    if mixer == "pool":
        out_specs.append(_const_spec((POOL_HALO, D)))
        out_shape.append(jax.ShapeDtypeStruct((POOL_HALO, D), F32))
        scratch.append(pltpu.VMEM((POOL_HALO + T, D), F32))
    for _, fc in FFN_CHUNKS:
        scratch += [pltpu.VMEM((FFN_HALO + T, fc), F32)] * 2
    return pl.pallas_call(
        functools.partial(_ffn_kernel, T=T, n_valid=n_valid, pos0=pos0, mixer=mixer,
                          final_norm=fn is not None),
        grid=(nb, L // T),
        in_specs=in_specs,
        out_specs=out_specs,
        out_shape=out_shape,
        scratch_shapes=scratch,
        compiler_params=_params(2),
        name="channel_mixer_" + mixer,
    )(*args)


def _row(v):
    return v.reshape(1, -1).astype(F32)


def _pad_rows(a, rows):
    return jnp.pad(a.astype(F32), ((0, rows - a.shape[0]), (0, 0)))


def _aug_constants():
    place = np.zeros((PARTS * LANES, 2 * LANES), np.float32)
    ones = np.zeros((1, 2 * LANES), np.float32)
    for h in range(FOX_HEADS):
        for j in range(PARTS):
            place[j * LANES + h, AUG_ONES + PARTS * h + j] = 1.0
            place[j * LANES + h, LANES + PARTS * h + j] = -1.0
            ones[0, PARTS * h + j] = 1.0
            ones[0, LANES + AUG_ONES + PARTS * h + j] = 1.0
    return jnp.asarray(place, BF16), jnp.asarray(ones, F32)


def kernel(x, meta_tokens, mix_norm_even, w_in, b_f, conv_w, conv_b, ln_g, ln_b, w_out,
           mix_norm_odd, pool_w, pool_b, pool_scale, ffn_norm, w_up, ffn_conv_w, ffn_conv_b,
           w_down, final_norm):
    W = FOX_WIDTH
    TM = META_PAD

    wi = w_in[0]
    w5 = jnp.concatenate([wi[:, 0:3 * W], wi[:, 3 * W + FOX_HEADS:]], axis=1).astype(BF16)
    wf = jnp.pad(wi[:, 3 * W:3 * W + FOX_HEADS], ((0, 0), (0, LANES - FOX_HEADS))).astype(BF16)
    bf_pad = jnp.pad(b_f[0].astype(F32), (0, LANES - FOX_HEADS)).reshape(1, LANES)
    place, ones = _aug_constants()
    wo = w_out[0].astype(BF16)
    pw = pool_w[0].astype(BF16)
    wup = w_up.astype(BF16)
    wd = w_down.astype(BF16)
    cw3 = [_pad_rows(ffn_conv_w[i], SUBLANES) for i in range(2)]

    hm = jnp.pad(meta_tokens.astype(F32), ((0, TM - N_META), (0, 0)))[None]
    hr = x.astype(F32)

    zero_c = jnp.zeros((1, LANES), F32)
    zero_u = jnp.zeros((CONV_HALO, CONV_CH), F32)
    zero_ffn = jnp.zeros((FFN_HALO, 2 * D_FF), F32)
    zero_pool = jnp.zeros((POOL_HALO, D_MODEL), F32)

    mi_args = (_row(mix_norm_even[0]), w5, wf, bf_pad)
    cm_args = (_pad_rows(conv_w[0], CONV_HALO), _row(conv_b[0]), _row(ln_g[0]), _row(ln_b[0]))
    qm, km, vm, zm, caq_m, cak_m, c_meta, u_meta = _mix_in(
        hm, *mi_args, zero_c, place, ones, zero_u, *cm_args, T=TM, n_valid=N_META)
    q, k, v, z, caq, cak, _, _ = _mix_in(
        hr, *mi_args, c_meta, place, ones, u_meta, *cm_args, T=T_MAIN, n_valid=T_MAIN)
    attn_m = _attention(qm, caq_m, km, cak_m, vm, TQ=TM)
    attn = _attention(q, caq, k, cak, v, meta=(km, cak_m, vm), TQ=T_ATTN)

    f_args = (_row(ffn_norm[0]), wup, cw3[0], _row(ffn_conv_b[0]), wd)
    hm, tail = _ffn(hm, "proj", (attn_m, zm, wo), 0, *f_args, zero_ffn,
                    T=TM, n_valid=N_META, pos0=0)
    hr, _ = _ffn(hr, "proj", (attn, z, wo), 0, *f_args, tail,
                 T=T_MAIN, n_valid=T_MAIN, pos0=N_META)

    f_args = (_row(ffn_norm[1]), wup, cw3[1], _row(ffn_conv_b[1]), wd)
    p_args = (pw, pool_b[0].reshape(1, -1).astype(F32), _row(pool_scale[0]))
    g1 = _row(mix_norm_odd[0])
    _, tail, ptail = _ffn(hm, "pool", (g1, zero_pool, *p_args), 1, *f_args, zero_ffn,
                          T=TM, n_valid=N_META, pos0=0)
    out, _, _ = _ffn(hr, "pool", (g1, ptail, *p_args), 1, *f_args, tail, _row(final_norm),
                     T=T_MAIN, n_valid=T_MAIN, pos0=N_META)
    return out.astype(x.dtype)
```
